```python
import math
import jax, jax.numpy as jnp
from jax import lax
import numpy as np

D_MODEL = 4096
BATCH = 4
SEQ = 2048
DEPTH = 4
DEC_BATCH = 32
DEC_SEQ = 1
PAST_LEN = 8192
PAGE_SIZE = 128

N_MIXERS = 3
HEAD_DIM = 128
ROT_DIM = HEAD_DIM // 4
ROPE_THETA = 500000.0
EPS = 1e-6
NEG_INF = -1e30
A_HEADS = D_MODEL // HEAD_DIM
A_KV_HEADS = 8
WINDOW = 128
BLOCK_Q = 128
B_HEADS = D_MODEL // (2 * HEAD_DIM)
B_KV_HEADS = 4
D_RNN = D_MODEL
RNN_BLOCKS = 16
RNN_BLOCK = D_RNN // RNN_BLOCKS
CONV_W = 4
LRU_C = 8.0
D_FF = -(-8 * D_MODEL // (3 * 256)) * 256

kernel_name = 'hybrid_swa_diff_rglru_adaln_step'


def _n_layers_of(kind):
    return len(range(kind, DEPTH, N_MIXERS))


def _rms_norm(x, g):
    xf = x.astype(jnp.float32)
    y = xf * lax.rsqrt(jnp.mean(xf * xf, axis=-1, keepdims=True) + EPS)
    return (y * g.astype(jnp.float32)).astype(x.dtype)


def _rope(x, pos):
    half = ROT_DIM // 2
    inv_freq = jnp.exp(-math.log(ROPE_THETA) * jnp.arange(half, dtype=jnp.float32) / half)
    ang = pos.astype(jnp.float32)[:, None] * inv_freq[None, :]
    cos = jnp.cos(ang)[:, None, :]
    sin = jnp.sin(ang)[:, None, :]
    xr = x[..., :ROT_DIM].astype(jnp.float32)
    x1, x2 = xr[..., :half], xr[..., half:]
    rot = jnp.concatenate([x1 * cos - x2 * sin, x2 * cos + x1 * sin], axis=-1).astype(x.dtype)
    return jnp.concatenate([rot, x[..., ROT_DIM:]], axis=-1)


def _ada(c, w, b):
    m = jnp.dot(jax.nn.silu(c), w) + b
    return m.reshape(c.shape[0], 6, D_MODEL)


def _modulate(x, g, shift, scale):
    return _rms_norm(x, g) * (1.0 + scale[:, None, :]) + shift[:, None, :]


def _swiglu(h, w_gu, w_down):
    gu = h @ w_gu
    return (jax.nn.silu(gu[..., :D_FF]) * gu[..., D_FF:]) @ w_down


def _swa_project(h, w_qkv, g_q, g_k, pos):
    B, S, _ = h.shape
    nq, nk = A_HEADS * HEAD_DIM, A_KV_HEADS * HEAD_DIM
    qkv = h @ w_qkv
    q = qkv[..., :nq].reshape(B, S, A_HEADS, HEAD_DIM)
    k = qkv[..., nq:nq + nk].reshape(B, S, A_KV_HEADS, HEAD_DIM)
    v = qkv[..., nq + nk:].reshape(B, S, A_KV_HEADS, HEAD_DIM)
    return _rope(_rms_norm(q, g_q), pos), _rope(_rms_norm(k, g_k), pos), v


def _swa_attend(q, k, v, qpos, kpos, sinks):
    N, Sq, Hq, Dh = q.shape
    Hkv = k.shape[2]
    G = Hq // Hkv
    qg = q.reshape(N, Sq, Hkv, G, Dh)
    s = jnp.einsum('nqhgd,nkhd->nhgqk', qg, k).astype(jnp.float32) * (Dh ** -0.5)
    dist = qpos[:, :, None] - kpos[:, None, :]
    valid = (dist >= 0) & (dist < WINDOW) & (kpos >= 0)[:, None, :]
    s = jnp.where(valid[:, None, None], s, NEG_INF)
    sink = sinks.astype(jnp.float32).reshape(1, Hkv, G, 1, 1)
    m = jnp.maximum(jnp.max(s, axis=-1, keepdims=True), sink)
    p = jnp.exp(s - m)
    w = p / (jnp.sum(p, axis=-1, keepdims=True) + jnp.exp(sink - m))
    o = jnp.einsum('nhgqk,nkhd->nqhgd', w.astype(v.dtype), v)
    return o.reshape(N, Sq, Hq, Dh)


def _swa_prompt(h, w_qkv, g_q, g_k, sinks, w_o):
    B, S, _ = h.shape
    nb = S // BLOCK_Q
    win = min(WINDOW, PAST_LEN)
    pos = jnp.arange(S)
    q, k, v = _swa_project(h, w_qkv, g_q, g_k, pos)

    def band(t):
        tp = jnp.pad(t, ((0, 0), (BLOCK_Q, 0), (0, 0), (0, 0)))
        tp = tp.reshape(B, nb + 1, BLOCK_Q, t.shape[2], t.shape[3])
        tb = jnp.concatenate([tp[:, :-1], tp[:, 1:]], axis=2)
        return tb.reshape(B * nb, 2 * BLOCK_Q, t.shape[2], t.shape[3])

    qb = q.reshape(B * nb, BLOCK_Q, A_HEADS, HEAD_DIM)
    starts = jnp.arange(nb) * BLOCK_Q
    qpos = jnp.tile(starts[:, None] + jnp.arange(BLOCK_Q)[None, :], (B, 1))
    kpos = jnp.tile(starts[:, None] - BLOCK_Q + jnp.arange(2 * BLOCK_Q)[None, :], (B, 1))
    o = _swa_attend(qb, band(k), band(v), qpos, kpos, sinks).reshape(B, S, A_HEADS * HEAD_DIM)
    return o @ w_o, k[:, S - win:], v[:, S - win:]


def _swa_sample(h, buf_k, buf_v, w_qkv, g_q, g_k, sinks, w_o):
    B, S, _ = h.shape
    W = buf_k.shape[1]
    qpos = PAST_LEN + jnp.arange(S)
    q, k, v = _swa_project(h, w_qkv, g_q, g_k, qpos)
    k_all = jnp.concatenate([buf_k, k], axis=1)
    v_all = jnp.concatenate([buf_v, v], axis=1)
    kpos = PAST_LEN - W + jnp.arange(W + S)
    o = _swa_attend(q, k_all, v_all, qpos[None], kpos[None], sinks).reshape(B, S, A_HEADS * HEAD_DIM)
    return o @ w_o, k_all[:, -W:], v_all[:, -W:]


def _diff_lambda(lq1, lk1, lq2, lk2, lam_init):
    f = jnp.float32
    return (jnp.exp(jnp.sum(lq1.astype(f) * lk1.astype(f)))
            - jnp.exp(jnp.sum(lq2.astype(f) * lk2.astype(f))) + lam_init)


def _diff_project(h, w_qkv, g_q, g_k, pos):
    B, S, _ = h.shape
    nq, nk = B_HEADS * 2 * HEAD_DIM, B_KV_HEADS * 2 * HEAD_DIM
    qkv = h @ w_qkv
    q = qkv[..., :nq].reshape(B, S, B_HEADS * 2, HEAD_DIM)
    k = qkv[..., nq:nq + nk].reshape(B, S, B_KV_HEADS * 2, HEAD_DIM)
    v = qkv[..., nq + nk:].reshape(B, S, B_KV_HEADS, 2 * HEAD_DIM)
    q = _rope(_rms_norm(q, g_q), pos)
    k = _rope(_rms_norm(k, g_k), pos).reshape(B, S, B_KV_HEADS, 2 * HEAD_DIM)
    return q, k, v


def _diff_attend(q, k, v, qpos, kpos, lam, lam_init, g_sub):
    B, Sq = q.shape[0], q.shape[1]
    Sk = k.shape[1]
    G = B_HEADS // B_KV_HEADS
    qg = q.reshape(B, Sq, B_KV_HEADS, G, 2, HEAD_DIM)
    kg = k.reshape(B, Sk, B_KV_HEADS, 2, HEAD_DIM)
    s = jnp.einsum('bqhgmd,bkhmd->bhgmqk', qg, kg).astype(jnp.float32) * (HEAD_DIM ** -0.5)
    causal = kpos[None, :] <= qpos[:, None]
    s = jnp.where(causal, s, NEG_INF)
    p = jax.nn.softmax(s, axis=-1)
    a = p[:, :, :, 0] - lam * p[:, :, :, 1]
    o = jnp.einsum('bhgqk,bkhe->bqhge', a.astype(v.dtype), v)
    o = _rms_norm(o, g_sub) * (1.0 - lam_init)
    return o.reshape(B, Sq, B_HEADS * 2 * HEAD_DIM)


def _diff_prompt(h, w_qkv, g_q, g_k, lam, lam_init, g_sub, w_o):
    B, S, _ = h.shape
    nb = S // BLOCK_Q
    pos = jnp.arange(S)
    q, k, v = _diff_project(h, w_qkv, g_q, g_k, pos)
    qb = q.reshape(B, nb, BLOCK_Q, B_HEADS * 2, HEAD_DIM).transpose(1, 0, 2, 3, 4)
    qposb = pos.reshape(nb, BLOCK_Q)
    ob = lax.map(lambda t: _diff_attend(t[0], k, v, t[1], pos, lam, lam_init, g_sub), (qb, qposb))
    o = ob.transpose(1, 0, 2, 3).reshape(B, S, B_HEADS * 2 * HEAD_DIM)
    return o @ w_o, k, v


def _diff_sample(h, pool_k, pool_v, page_table, w_qkv, g_q, g_k, lam, lam_init, g_sub, w_o):
    B, S, _ = h.shape
    qpos = PAST_LEN + jnp.arange(S)
    q, k, v = _diff_project(h, w_qkv, g_q, g_k, qpos)
    past_k = pool_k[page_table].reshape(B, -1, B_KV_HEADS, 2 * HEAD_DIM)
    past_v = pool_v[page_table].reshape(B, -1, B_KV_HEADS, 2 * HEAD_DIM)
    k_all = jnp.concatenate([past_k, k], axis=1)
    v_all = jnp.concatenate([past_v, v], axis=1)
    kpos = jnp.arange(k_all.shape[1])
    o = _diff_attend(q, k_all, v_all, qpos, kpos, lam, lam_init, g_sub)
    return o @ w_o, k, v


def _lru_block(h, conv_buf, h0, w_in, conv_w, conv_b, wa, ba, wx, bx, a_param, w_out):
    B, S, _ = h.shape
    gy = h @ w_in
    y = jax.nn.gelu(gy[..., :D_RNN], approximate=True)
    xb = gy[..., D_RNN:]
    xpad = jnp.concatenate([conv_buf.astype(xb.dtype), xb], axis=1)
    xc = conv_b + xpad[:, 0:S] * conv_w[0]
    for t in range(1, CONV_W):
        xc = xc + xpad[:, t:t + S] * conv_w[t]
    xblk = xc.reshape(B, S, RNN_BLOCKS, RNN_BLOCK)
    r = jax.nn.sigmoid(jnp.einsum('bsnd,nde->bsne', xblk, wa).reshape(B, S, D_RNN) + ba)
    ig = jax.nn.sigmoid(jnp.einsum('bsnd,nde->bsne', xblk, wx).reshape(B, S, D_RNN) + bx)
    log_a = (-LRU_C * r.astype(jnp.float32)) * jax.nn.softplus(-a_param.astype(jnp.float32))
    a = jnp.exp(log_a)
    bterm = jnp.sqrt(-jnp.expm1(2.0 * log_a)) * (ig * xc).astype(jnp.float32)

    def step(hc, ab):
        hn = ab[0] * hc + ab[1]
        return hn, hn

    hT, hs = lax.scan(step, h0.astype(jnp.float32), (a.swapaxes(0, 1), bterm.swapaxes(0, 1)))
    hs = hs.swapaxes(0, 1).astype(h.dtype)
    out = (hs * y) @ w_out
    return out, hT.astype(h0.dtype), xpad[:, -(CONV_W - 1):]


def setup_inputs(seed: int = 0):
    key = jax.random.key(seed)
    keys = list(jax.random.split(key, 64))
    f32 = jnp.float32

    def nk():
        return keys.pop()

    def normal(shape, scale=1.0):
        return jax.random.normal(nk(), shape, f32) * scale

    def gain(shape):
        return 1.0 + 0.02 * normal(shape)

    n_a, n_b, n_c = _n_layers_of(0), _n_layers_of(1), _n_layers_of(2)
    n_pages = PAST_LEN // PAGE_SIZE
    n_pool = (5 * DEC_BATCH * n_pages + 3) // 4
    win = min(WINDOW, PAST_LEN)
    qkv_a = (A_HEADS + 2 * A_KV_HEADS) * HEAD_DIM
    qkv_b = (2 * B_HEADS + 4 * B_KV_HEADS) * HEAD_DIM

    page_table = jax.random.permutation(nk(), n_pool)[:DEC_BATCH * n_pages]
    page_table = page_table.reshape(DEC_BATCH, n_pages).astype(jnp.int32)
    u = jax.random.uniform(nk(), (n_c, D_RNN), f32, 0.9, 0.999)
    s = u ** (1.0 / LRU_C)
    a_param = jnp.log(s) - jnp.log1p(-s)

    return {
        'x_prompt': normal((BATCH, SEQ, D_MODEL)),
        'x_sample': normal((DEC_BATCH, DEC_SEQ, D_MODEL)),
        'cache_a_k': normal((n_a, DEC_BATCH, win, A_KV_HEADS, HEAD_DIM)),
        'cache_a_v': normal((n_a, DEC_BATCH, win, A_KV_HEADS, HEAD_DIM)),
        'cache_b_k': normal((n_b, n_pool, PAGE_SIZE, B_KV_HEADS, 2 * HEAD_DIM)),
        'cache_b_v': normal((n_b, n_pool, PAGE_SIZE, B_KV_HEADS, 2 * HEAD_DIM)),
        'state_lru_h': normal((n_c, DEC_BATCH, D_RNN), 0.5),
        'state_lru_conv': normal((n_c, DEC_BATCH, CONV_W - 1, D_RNN)),
        'page_table': page_table,
        'c_prompt': normal((BATCH, D_MODEL)),
        'c_sample': normal((DEC_BATCH, D_MODEL)),
        'w_ada': normal((DEPTH, D_MODEL, 6 * D_MODEL), 0.5 * D_MODEL ** -0.5),
        'b_ada': normal((DEPTH, 6 * D_MODEL), 0.01),
        'g_norm_mix': gain((DEPTH, D_MODEL)),
        'g_norm_ffn': gain((DEPTH, D_MODEL)),
        'a_w_qkv': normal((n_a, D_MODEL, qkv_a), D_MODEL ** -0.5),
        'a_g_q': gain((n_a, HEAD_DIM)),
        'a_g_k': gain((n_a, HEAD_DIM)),
        'a_sinks': normal((n_a, A_HEADS)),
        'a_w_o': normal((n_a, A_HEADS * HEAD_DIM, D_MODEL), (A_HEADS * HEAD_DIM) ** -0.5),
        'b_w_qkv': normal((n_b, D_MODEL, qkv_b), D_MODEL ** -0.5),
        'b_g_q': gain((n_b, HEAD_DIM)),
        'b_g_k': gain((n_b, HEAD_DIM)),
        'b_lam_q1': normal((n_b, HEAD_DIM), 0.1),
        'b_lam_k1': normal((n_b, HEAD_DIM), 0.1),
        'b_lam_q2': normal((n_b, HEAD_DIM), 0.1),
        'b_lam_k2': normal((n_b, HEAD_DIM), 0.1),
        'b_g_sub': gain((n_b, 2 * HEAD_DIM)),
        'b_w_o': normal((n_b, B_HEADS * 2 * HEAD_DIM, D_MODEL), (B_HEADS * 2 * HEAD_DIM) ** -0.5),
        'lru_w_in': normal((n_c, D_MODEL, 2 * D_RNN), D_MODEL ** -0.5),
        'lru_conv_w': normal((n_c, CONV_W, D_RNN), CONV_W ** -0.5),
        'lru_conv_b': normal((n_c, D_RNN), 0.01),
        'lru_wa': normal((n_c, RNN_BLOCKS, RNN_BLOCK, RNN_BLOCK), RNN_BLOCK ** -0.5),
        'lru_ba': normal((n_c, D_RNN), 0.01),
        'lru_wx': normal((n_c, RNN_BLOCKS, RNN_BLOCK, RNN_BLOCK), RNN_BLOCK ** -0.5),
        'lru_bx': normal((n_c, D_RNN), 0.01),
        'lru_a_param': a_param,
        'lru_w_out': normal((n_c, D_RNN, D_MODEL), D_RNN ** -0.5),
        'ffn_w_gu': normal((DEPTH, D_MODEL, 2 * D_FF), D_MODEL ** -0.5),
        'ffn_w_down': normal((DEPTH, D_FF, D_MODEL), D_FF ** -0.5),
    }


def reference(x_prompt, x_sample, cache_a_k, cache_a_v, cache_b_k, cache_b_v,
              state_lru_h, state_lru_conv, page_table, c_prompt, c_sample,
              w_ada, b_ada, g_norm_mix, g_norm_ffn,
              a_w_qkv, a_g_q, a_g_k, a_sinks, a_w_o,
              b_w_qkv, b_g_q, b_g_k, b_lam_q1, b_lam_k1, b_lam_q2, b_lam_k2, b_g_sub, b_w_o,
              lru_w_in, lru_conv_w, lru_conv_b, lru_wa, lru_ba, lru_wx, lru_bx, lru_a_param, lru_w_out,
              ffn_w_gu, ffn_w_down):
    xp, xs = x_prompt, x_sample
    a_kp, a_vp, a_ks, a_vs = [], [], [], []
    b_kp, b_vp, b_ks, b_vs = [], [], [], []
    l_hp, l_cp, l_hs, l_cs = [], [], [], []
    for i in range(DEPTH):
        kind, j = i % N_MIXERS, i // N_MIXERS
        mp = _ada(c_prompt, w_ada[i], b_ada[i])
        ms = _ada(c_sample, w_ada[i], b_ada[i])
        hp = _modulate(xp, g_norm_mix[i], mp[:, 0], mp[:, 1])
        hs = _modulate(xs, g_norm_mix[i], ms[:, 0], ms[:, 1])
        if kind == 0:
            op, kp, vp = _swa_prompt(hp, a_w_qkv[j], a_g_q[j], a_g_k[j], a_sinks[j], a_w_o[j])
            os_, ks_, vs_ = _swa_sample(hs, cache_a_k[j], cache_a_v[j], a_w_qkv[j], a_g_q[j],
                                        a_g_k[j], a_sinks[j], a_w_o[j])
            a_kp.append(kp); a_vp.append(vp); a_ks.append(ks_); a_vs.append(vs_)
        elif kind == 1:
            lam_init = 0.8 - 0.6 * math.exp(-0.3 * i)
            lam = _diff_lambda(b_lam_q1[j], b_lam_k1[j], b_lam_q2[j], b_lam_k2[j], lam_init)
            op, kp, vp = _diff_prompt(hp, b_w_qkv[j], b_g_q[j], b_g_k[j], lam, lam_init,
                                      b_g_sub[j], b_w_o[j])
            os_, ks_, vs_ = _diff_sample(hs, cache_b_k[j], cache_b_v[j], page_table, b_w_qkv[j],
                                         b_g_q[j], b_g_k[j], lam, lam_init, b_g_sub[j], b_w_o[j])
            b_kp.append(kp); b_vp.append(vp); b_ks.append(ks_); b_vs.append(vs_)
        else:
            h0 = jnp.zeros((xp.shape[0], D_RNN), state_lru_h.dtype)
            cb0 = jnp.zeros((xp.shape[0], CONV_W - 1, D_RNN), state_lru_conv.dtype)
            op, hTp, cbp = _lru_block(hp, cb0, h0, lru_w_in[j], lru_conv_w[j], lru_conv_b[j],
                                      lru_wa[j], lru_ba[j], lru_wx[j], lru_bx[j],
                                      lru_a_param[j], lru_w_out[j])
            os_, hTs, cbs = _lru_block(hs, state_lru_conv[j], state_lru_h[j], lru_w_in[j],
                                       lru_conv_w[j], lru_conv_b[j], lru_wa[j], lru_ba[j],
                                       lru_wx[j], lru_bx[j], lru_a_param[j], lru_w_out[j])
            l_hp.append(hTp); l_cp.append(cbp); l_hs.append(hTs); l_cs.append(cbs)
        xp = xp + mp[:, 2, None, :] * op
        xs = xs + ms[:, 2, None, :] * os_
        hp = _modulate(xp, g_norm_ffn[i], mp[:, 3], mp[:, 4])
        hs = _modulate(xs, g_norm_ffn[i], ms[:, 3], ms[:, 4])
        xp = xp + mp[:, 5, None, :] * _swiglu(hp, ffn_w_gu[i], ffn_w_down[i])
        xs = xs + ms[:, 5, None, :] * _swiglu(hs, ffn_w_gu[i], ffn_w_down[i])
    return (xp, xs,
            jnp.stack(a_kp), jnp.stack(a_vp), jnp.stack(a_ks), jnp.stack(a_vs),
            jnp.stack(b_kp), jnp.stack(b_vp), jnp.stack(b_ks), jnp.stack(b_vs),
            jnp.stack(l_hp), jnp.stack(l_cp), jnp.stack(l_hs), jnp.stack(l_cs))
```

```python
import functools
import math

import jax
import jax.numpy as jnp
from jax import lax
from jax.experimental import pallas as pl
from jax.experimental.pallas import tpu as pltpu

F32 = jnp.float32
BF16 = jnp.bfloat16

DEPTH = 4
N_MIXERS = 3
HEAD_DIM = 128
ROT_DIM = HEAD_DIM // 4
ROPE_THETA = 500000.0
EPS = 1e-6
NEG_INF = -1e30
WINDOW = 128
PAST_LEN = 8192
LRU_C = 8.0
CONV_W = 4
SOFTMAX_SCALE = HEAD_DIM ** -0.5

LANES = 128
V7X_VMEM_REQUEST_CAP = 60 << 20
PAGES_PER_STEP = 4


def _compiler_params(semantics, vmem_bytes):
    limit = min(max(int(vmem_bytes * 1.25), 32 << 20), V7X_VMEM_REQUEST_CAP)
    return pltpu.CompilerParams(dimension_semantics=semantics, vmem_limit_bytes=limit)


def _mm_body(*refs, n_w, n_extra, nk, tk, epilogue, x_transform):
    x_ref = refs[0]
    w_refs = refs[1:1 + n_w]
    extra_refs = refs[1 + n_w:1 + n_w + n_extra]
    o_ref = refs[1 + n_w + n_extra]
    acc_refs = refs[2 + n_w + n_extra:]
    n = pl.program_id(1)
    k = pl.program_id(2)
    if nk == 1:
        xs = x_ref[...]
    else:
        xs = x_ref[:, pl.ds(pl.multiple_of(k * tk, LANES), tk)]
    if x_transform is not None:
        xs = x_transform(xs)
    xs = xs.astype(BF16)
    parts = [jnp.dot(xs, w[...].astype(BF16), preferred_element_type=F32) for w in w_refs]
    if nk == 1:
        epilogue(parts, n, extra_refs, o_ref)
        return

    @pl.when(k == 0)
    def _():
        for a, p in zip(acc_refs, parts):
            a[...] = p

    @pl.when(jnp.logical_and(k > 0, k < nk - 1))
    def _():
        for a, p in zip(acc_refs, parts):
            a[...] += p

    @pl.when(k == nk - 1)
    def _():
        epilogue([a[...] + p for a, p in zip(acc_refs, parts)], n, extra_refs, o_ref)


def _matmul(x, weights, extras, epilogue, n_out, out_dtype, *, tm, tn, tk=None,
            x_transform=None, name):
    M, K = x.shape
    tk = K if tk is None else tk
    assert M % tm == 0 and n_out % tn == 0 and K % tk == 0
    nk = K // tk
    grid = (M // tm, n_out // tn, nk)
    x_bytes = tm * K * x.dtype.itemsize
    x_buffers = 1 if x_bytes > (16 << 20) else 2
    x_spec = pl.BlockSpec((tm, K), lambda m, n, k: (m, 0),
                          pipeline_mode=pl.Buffered(1) if x_buffers == 1 else None)
    in_specs = [x_spec]
    operands = [x]
    for w, off in weights:
        in_specs.append(pl.BlockSpec((tk, tn), functools.partial(
            lambda m, n, k, off: (k, n + off), off=off)))
        operands.append(w)
    for arr, spec in extras:
        in_specs.append(spec)
        operands.append(arr)
    n_w = len(weights)
    scratch = [pltpu.VMEM((tm, tn), F32) for _ in range(n_w)] if nk > 1 else []
    out_itemsize = jnp.dtype(out_dtype).itemsize
    vmem = (x_bytes * x_buffers
            + n_w * tk * tn * (2 * 4 + 2)
            + tm * tn * (2 * out_itemsize + 4 * (2 * n_w + 2))
            + sum(2 * _block_bytes(a, s) for a, s in extras))
    body = functools.partial(_mm_body, n_w=n_w, n_extra=len(extras), nk=nk, tk=tk,
                             epilogue=epilogue, x_transform=x_transform)
    return pl.pallas_call(
        body,
        grid=grid,
        in_specs=in_specs,
        out_specs=pl.BlockSpec((tm, tn), lambda m, n, k: (m, n)),
        out_shape=jax.ShapeDtypeStruct((M, n_out), out_dtype),
        scratch_shapes=scratch,
        compiler_params=_compiler_params(("parallel", "parallel", "arbitrary"), vmem),
        name=name,
    )(*operands)


def _block_bytes(arr, spec):
    return math.prod(1 if d is None else d for d in spec.block_shape) * arr.dtype.itemsize


def _epi_bias(parts, n, extras, o_ref):
    (b_ref,) = extras
    o_ref[...] = parts[0] + b_ref[...]


def _epi_swiglu(parts, n, extras, o_ref):
    g, u = parts
    o_ref[...] = (jax.nn.silu(g) * u).astype(o_ref.dtype)


def _epi_residual(parts, n, extras, o_ref):
    res_ref, gate_ref = extras
    o_ref[...] = res_ref[...] + gate_ref[...] * parts[0]


def _epi_gelu_first_tiles(parts, n, extras, o_ref, *, n_gelu_tiles):
    acc = parts[0]

    @pl.when(n < n_gelu_tiles)
    def _():
        o_ref[...] = jax.nn.gelu(acc, approximate=True)

    @pl.when(n >= n_gelu_tiles)
    def _():
        o_ref[...] = acc


def _epi_qk_norm_rope(parts, n, extras, o_ref, *, n_qk_tiles):
    gain_ref, cos_ref, sin_lo_ref, sin_hi_ref = extras
    acc = parts[0]
    half = ROT_DIM // 2

    @pl.when(n < n_qk_tiles)
    def _():
        cos = cos_ref[...]
        sin_lo = sin_lo_ref[...]
        sin_hi = sin_hi_ref[...]
        for c in range(acc.shape[1] // HEAD_DIM):
            cols = slice(c * HEAD_DIM, (c + 1) * HEAD_DIM)
            y = acc[:, cols]
            y = y * lax.rsqrt(jnp.mean(y * y, axis=-1, keepdims=True) + EPS)
            y = y * gain_ref[:, cols]
            o_ref[:, cols] = (y * cos + pltpu.roll(y, HEAD_DIM - half, 1) * sin_lo
                              + pltpu.roll(y, half, 1) * sin_hi)

    @pl.when(n >= n_qk_tiles)
    def _():
        o_ref[...] = acc


def _modulate_body(x_ref, g_ref, scale_ref, shift_ref, o_ref):
    x = x_ref[...]
    y = x * lax.rsqrt(jnp.mean(x * x, axis=-1, keepdims=True) + EPS)
    y = y * g_ref[...]
    o_ref[...] = (y * (1.0 + scale_ref[...]) + shift_ref[...]).astype(o_ref.dtype)


def _modulate(path, x, g, j_shift, j_scale):
    M, D = x.shape
    tm = path["tm_rows"]
    mod = path["mod"]
    r = mod.shape[1]
    mod_spec = lambda j: pl.BlockSpec((None, r, D), lambda m: path["mod_index"](j, m * tm) + (0,))
    return pl.pallas_call(
        _modulate_body,
        grid=(M // tm,),
        in_specs=[pl.BlockSpec((tm, D), lambda m: (m, 0)),
                  pl.BlockSpec((1, D), lambda m: (0, 0)),
                  mod_spec(j_scale), mod_spec(j_shift)],
        out_specs=pl.BlockSpec((tm, D), lambda m: (m, 0)),
        out_shape=jax.ShapeDtypeStruct((M, D), BF16),
        compiler_params=_compiler_params(("parallel",), tm * D * (2 * 4 + 2 * 2 + 8)),
        name="modulate",
    )(x, g.reshape(1, D), mod, mod)


def _swa_prompt_body(sinks_ref, q_ref, kp_ref, kc_ref, vp_ref, vc_ref, o_ref, *, group):
    i = pl.program_id(1)
    g = pl.program_id(2)
    tq = q_ref.shape[0]
    kk = jnp.concatenate([kp_ref[...], kc_ref[...]], axis=0).astype(BF16)
    vv = jnp.concatenate([vp_ref[...], vc_ref[...]], axis=0).astype(BF16)
    row = lax.broadcasted_iota(jnp.int32, (tq, 2 * tq), 0)
    col = lax.broadcasted_iota(jnp.int32, (tq, 2 * tq), 1)
    qpos = i * tq + row
    kpos = (i - 1) * tq + col
    dist = qpos - kpos
    valid = (dist >= 0) & (dist < WINDOW) & (kpos >= 0)
    for h in range(group):
        cols = slice(h * HEAD_DIM, (h + 1) * HEAD_DIM)
        qh = q_ref[:, cols].astype(BF16)
        s = lax.dot_general(qh, kk, (((1,), (1,)), ((), ())),
                            preferred_element_type=F32) * SOFTMAX_SCALE
        s = jnp.where(valid, s, NEG_INF)
        sink = sinks_ref[g * group + h]
        m = jnp.maximum(jnp.max(s, axis=-1, keepdims=True), sink)
        p = jnp.exp(s - m)
        w = p / (jnp.sum(p, axis=-1, keepdims=True) + jnp.exp(sink - m))
        o_ref[:, cols] = jnp.dot(w.astype(BF16), vv,
                                 preferred_element_type=F32).astype(o_ref.dtype)


def _swa_prompt(qkv, sinks, batch, seq, n_heads, n_kv):
    tq = WINDOW
    nq = seq // tq
    group = n_heads // n_kv
    k0 = n_heads
    v0 = n_heads + n_kv
    cur = lambda off: (lambda b, i, g: (b * nq + i, off + g))
    prev = lambda off: (lambda b, i, g: (jnp.maximum(b * nq + i - 1, 0), off + g))
    return pl.pallas_call(
        functools.partial(_swa_prompt_body, group=group),
        grid=(batch, nq, n_kv),
        in_specs=[pl.BlockSpec(memory_space=pltpu.SMEM),
                  pl.BlockSpec((tq, group * HEAD_DIM), lambda b, i, g: (b * nq + i, g)),
                  pl.BlockSpec((tq, HEAD_DIM), prev(k0)),
                  pl.BlockSpec((tq, HEAD_DIM), cur(k0)),
                  pl.BlockSpec((tq, HEAD_DIM), prev(v0)),
                  pl.BlockSpec((tq, HEAD_DIM), cur(v0))],
        out_specs=pl.BlockSpec((tq, group * HEAD_DIM), lambda b, i, g: (b * nq + i, g)),
        out_shape=jax.ShapeDtypeStruct((batch * seq, n_heads * HEAD_DIM), BF16),
        compiler_params=_compiler_params(("parallel", "parallel", "parallel"), 8 << 20),
        name="swa_prompt",
    )(sinks, qkv, qkv, qkv, qkv, qkv)


def _swa_sample_body(sinks_ref, q_ref, kn_ref, vn_ref, ck_ref, cv_ref, o_ref, *, group):
    n_heads = q_ref.shape[0]
    n_kv = kn_ref.shape[0]
    w_buf = ck_ref.shape[0]
    q = q_ref[...].astype(BF16)
    qf = q.astype(F32)
    sink = sinks_ref[...]
    col = lax.broadcasted_iota(jnp.int32, (n_heads, w_buf), 1)
    head_group = lax.broadcasted_iota(jnp.int32, (n_heads, HEAD_DIM), 0) // group
    kpos = PAST_LEN - w_buf + col
    dist = PAST_LEN - kpos
    valid = (dist >= 0) & (dist < WINDOW) & (kpos >= 0)
    out = jnp.zeros((n_heads, HEAD_DIM), F32)
    for g in range(n_kv):
        cols = slice(g * HEAD_DIM, (g + 1) * HEAD_DIM)
        kc = ck_ref[:, cols].astype(BF16)
        vc = cv_ref[:, cols].astype(BF16)
        kn = kn_ref[g:g + 1, :].astype(BF16).astype(F32)
        vn = vn_ref[g:g + 1, :].astype(BF16).astype(F32)
        s_c = lax.dot_general(q, kc, (((1,), (1,)), ((), ())),
                              preferred_element_type=F32) * SOFTMAX_SCALE
        s_c = jnp.where(valid, s_c, NEG_INF)
        s_n = jnp.sum(qf * kn, axis=-1, keepdims=True) * SOFTMAX_SCALE
        m = jnp.maximum(jnp.maximum(jnp.max(s_c, axis=-1, keepdims=True), s_n), sink)
        p_c = jnp.exp(s_c - m)
        p_n = jnp.exp(s_n - m)
        denom = jnp.sum(p_c, axis=-1, keepdims=True) + p_n + jnp.exp(sink - m)
        w_c = (p_c / denom).astype(BF16)
        w_n = (p_n / denom).astype(BF16).astype(F32)
        o_g = jnp.dot(w_c, vc, preferred_element_type=F32) + w_n * vn
        out = jnp.where(head_group == g, o_g, out)
    o_ref[...] = out.astype(o_ref.dtype)


def _swa_sample(qkv, cache_k, cache_v, sinks, n_heads, n_kv):
    B = qkv.shape[0]
    w_buf = cache_k.shape[1]
    nq, nk = n_heads * HEAD_DIM, n_kv * HEAD_DIM
    q3 = qkv[:, :nq].reshape(B, n_heads, HEAD_DIM)
    kn3 = qkv[:, nq:nq + nk].reshape(B, n_kv, HEAD_DIM)
    vn3 = qkv[:, nq + nk:].reshape(B, n_kv, HEAD_DIM)
    per_b = lambda r, c: pl.BlockSpec((None, r, c), lambda b: (b, 0, 0))
    o3 = pl.pallas_call(
        functools.partial(_swa_sample_body, group=n_heads // n_kv),
        grid=(B,),
        in_specs=[pl.BlockSpec((n_heads, 1), lambda b: (0, 0)),
                  per_b(n_heads, HEAD_DIM), per_b(n_kv, HEAD_DIM), per_b(n_kv, HEAD_DIM),
                  per_b(w_buf, nk), per_b(w_buf, nk)],
        out_specs=per_b(n_heads, HEAD_DIM),
        out_shape=jax.ShapeDtypeStruct((B, n_heads, HEAD_DIM), BF16),
        compiler_params=_compiler_params(("parallel",), 8 << 20),
        name="swa_sample",
    )(sinks.reshape(n_heads, 1), q3, kn3, vn3,
      cache_k.reshape(B, w_buf, nk), cache_v.reshape(B, w_buf, nk))
    return o3.reshape(B, nq), kn3, vn3


def _diff_lambda(lamv_ref, lam_init):
    d1 = jnp.sum(lamv_ref[0:1, :] * lamv_ref[1:2, :], axis=-1, keepdims=True)
    d2 = jnp.sum(lamv_ref[2:3, :] * lamv_ref[3:4, :], axis=-1, keepdims=True)
    return jnp.exp(d1) - jnp.exp(d2) + lam_init


def _sub_norm(o, gsub, lam_init):
    o = o * lax.rsqrt(jnp.mean(o * o, axis=-1, keepdims=True) + EPS)
    return (o * gsub) * (1.0 - lam_init)


def _diff_prompt_body(lamv_ref, gsub_ref, q_ref, k_ref, v_ref, o_ref, k1_s, k2_s, v_s,
                      *, group, lam_init):
    i = pl.program_id(2)
    tq = q_ref.shape[0]
    sk = k_ref.shape[0]

    @pl.when(i == 0)
    def _():
        k1_s[...] = k_ref[:, :HEAD_DIM].astype(BF16)
        k2_s[...] = k_ref[:, HEAD_DIM:].astype(BF16)
        v_s[...] = v_ref[...].astype(BF16)

    lam = _diff_lambda(lamv_ref, lam_init)
    row = lax.broadcasted_iota(jnp.int32, (tq, sk), 0) + i * tq
    col = lax.broadcasted_iota(jnp.int32, (tq, sk), 1)
    causal = col <= row
    vb = v_s[...]
    for g in range(group):
        probs = []
        for m, ks in ((0, k1_s), (1, k2_s)):
            c0 = (2 * g + m) * HEAD_DIM
            qm = q_ref[:, c0:c0 + HEAD_DIM].astype(BF16)
            s = lax.dot_general(qm, ks[...], (((1,), (1,)), ((), ())),
                                preferred_element_type=F32) * SOFTMAX_SCALE
            s = jnp.where(causal, s, NEG_INF)
            e = jnp.exp(s - jnp.max(s, axis=-1, keepdims=True))
            probs.append(e / jnp.sum(e, axis=-1, keepdims=True))
        a = probs[0] - lam * probs[1]
        o = jnp.dot(a.astype(BF16), vb, preferred_element_type=F32)
        o = _sub_norm(o, gsub_ref[...], lam_init)
        o_ref[:, g * 2 * HEAD_DIM:(g + 1) * 2 * HEAD_DIM] = o.astype(o_ref.dtype)


def _diff_prompt(qkv, lamv, gsub, lam_init, batch, seq, n_heads, n_kv):
    tq = WINDOW
    nq = seq // tq
    group = n_heads // n_kv
    dv = 2 * HEAD_DIM
    k0 = n_heads
    v0 = n_heads + n_kv
    return pl.pallas_call(
        functools.partial(_diff_prompt_body, group=group, lam_init=lam_init),
        grid=(batch, n_kv, nq),
        in_specs=[pl.BlockSpec((4, HEAD_DIM), lambda b, h, i: (0, 0)),
                  pl.BlockSpec((1, dv), lambda b, h, i: (0, 0)),
                  pl.BlockSpec((tq, group * dv), lambda b, h, i: (b * nq + i, h)),
                  pl.BlockSpec((seq, dv), lambda b, h, i: (b, k0 + h)),
                  pl.BlockSpec((seq, dv), lambda b, h, i: (b, v0 + h))],
        out_specs=pl.BlockSpec((tq, group * dv), lambda b, h, i: (b * nq + i, h)),
        out_shape=jax.ShapeDtypeStruct((batch * seq, n_heads * dv), BF16),
        scratch_shapes=[pltpu.VMEM((seq, HEAD_DIM), BF16), pltpu.VMEM((seq, HEAD_DIM), BF16),
                        pltpu.VMEM((seq, dv), BF16)],
        compiler_params=_compiler_params(("parallel", "parallel", "arbitrary"), 40 << 20),
        name="diff_prompt",
    )(lamv, gsub, qkv, qkv, qkv)


def _diff_sample_body(pt_ref, lamv_ref, gsub_ref, q1_ref, q2_ref, kn_ref, vn_ref, *rest,
                      n_kv, group, lam_init, n_steps):
    pages = PAGES_PER_STEP
    k_refs = rest[:pages]
    v_refs = rest[pages:2 * pages]
    o_ref = rest[2 * pages]
    s1_s, s2_s, a_s, anew_s, acc_s = rest[2 * pages + 1:]
    sweep = pl.program_id(1)
    j = pl.program_id(2)
    psz = k_refs[0].shape[0]
    dv = 2 * HEAD_DIM
    first_row = lax.broadcasted_iota(jnp.int32, (psz, n_kv * dv), 0) == 0

    @pl.when(sweep == 0)
    def _():
        q1 = q1_ref[...]
        q2 = q2_ref[...]
        for i in range(pages):
            rows = pl.ds(pl.multiple_of((j * pages + i) * psz, psz), psz)
            kp = k_refs[i][...].astype(BF16)
            s1_s[rows, :] = jnp.dot(kp, q1, preferred_element_type=F32)
            s2_s[rows, :] = jnp.dot(kp, q2, preferred_element_type=F32)

    @pl.when(jnp.logical_and(sweep == 1, j == 0))
    def _():
        lam = _diff_lambda(lamv_ref, lam_init)
        kn = jnp.where(first_row, kn_ref[...], 0.0).astype(BF16)
        normed = []
        for s_s, q_ref in ((s1_s, q1_ref), (s2_s, q2_ref)):
            s = s_s[...] * SOFTMAX_SCALE
            s_n = jnp.dot(kn, q_ref[...], preferred_element_type=F32)[0:1, :] * SOFTMAX_SCALE
            m = jnp.maximum(jnp.max(s, axis=0, keepdims=True), s_n)
            e = jnp.exp(s - m)
            e_n = jnp.exp(s_n - m)
            denom = jnp.sum(e, axis=0, keepdims=True) + e_n
            normed.append((e / denom, e_n / denom))
        a_s[...] = normed[0][0] - lam * normed[1][0]
        a_n = normed[0][1] - lam * normed[1][1]
        anew_s[...] = jnp.where(lax.broadcasted_iota(jnp.int32, anew_s.shape, 0) == 0, a_n, 0.0)
        acc_s[...] = jnp.zeros_like(acc_s)

    @pl.when(sweep == 1)
    def _():
        for i in range(pages):
            rows = pl.ds(pl.multiple_of((j * pages + i) * psz, psz), psz)
            a_t = a_s[rows, :].T.astype(BF16)
            acc_s[...] += jnp.dot(a_t, v_refs[i][...].astype(BF16), preferred_element_type=F32)

    @pl.when(jnp.logical_and(sweep == 1, j == n_steps - 1))
    def _():
        vn = jnp.where(first_row, vn_ref[...], 0.0).astype(BF16)
        acc = acc_s[...] + jnp.dot(anew_s[...].T.astype(BF16), vn, preferred_element_type=F32)
        for h in range(n_kv):
            o = acc[h * group:(h + 1) * group, h * dv:(h + 1) * dv]
            o_ref[h * group:(h + 1) * group, :] = _sub_norm(o, gsub_ref[...], lam_init)


def _diff_sample(qkv, pool_k, pool_v, page_table, lamv, gsub, lam_init, n_heads, n_kv):
    B = qkv.shape[0]
    n_pool, psz = pool_k.shape[0], pool_k.shape[1]
    n_pages = page_table.shape[1]
    pages = PAGES_PER_STEP
    assert n_pages % pages == 0 and psz == LANES
    n_steps = n_pages // pages
    group = n_heads // n_kv
    dv = 2 * HEAD_DIM
    nq, nk = n_heads * dv, n_kv * dv
    q6 = qkv[:, :nq].reshape(B, n_kv, group, 2, HEAD_DIM)
    eye = jnp.eye(n_kv, dtype=F32)
    qmaps = []
    for m in range(2):
        blk = jnp.einsum("bhgd,hH->bhdHg", q6[:, :, :, m, :], eye)
        full = jnp.zeros((B, n_kv, 2, HEAD_DIM, n_kv * group), F32)
        full = full.at[:, :, m].set(blk.reshape(B, n_kv, HEAD_DIM, n_kv * group))
        full = full.reshape(B, nk, n_kv * group)
        qmaps.append(jnp.pad(full, ((0, 0), (0, 0), (0, LANES - n_kv * group))).astype(BF16))
    kn = qkv[:, nq:nq + nk].reshape(B, 1, nk)
    vn = qkv[:, nq + nk:].reshape(B, 1, nk)

    def k_page(i):
        def index(b, s, j, pt):
            step = j * (1 - s) + (n_steps - 1) * s
            return (pt[b, step * pages + i], 0, 0)
        return pl.BlockSpec((None, psz, nk), index)

    def v_page(i):
        def index(b, s, j, pt):
            return (pt[b, j * s * pages + i], 0, 0)
        return pl.BlockSpec((None, psz, nk), index)

    per_b = lambda r, c: pl.BlockSpec((None, r, c), lambda b, s, j, pt: (b, 0, 0))
    const = lambda r, c: pl.BlockSpec((r, c), lambda b, s, j, pt: (0, 0))
    n_keys = n_pages * psz
    grid_spec = pltpu.PrefetchScalarGridSpec(
        num_scalar_prefetch=1,
        grid=(B, 2, n_steps),
        in_specs=[const(4, HEAD_DIM), const(1, dv), per_b(nk, LANES), per_b(nk, LANES),
                  per_b(1, nk), per_b(1, nk)]
                 + [k_page(i) for i in range(pages)] + [v_page(i) for i in range(pages)],
        out_specs=per_b(n_heads, dv),
        scratch_shapes=[pltpu.VMEM((n_keys, LANES), F32), pltpu.VMEM((n_keys, LANES), F32),
                        pltpu.VMEM((n_keys, LANES), F32), pltpu.VMEM((psz, LANES), F32),
                        pltpu.VMEM((LANES, nk), F32)])
    pk = pool_k.reshape(n_pool, psz, nk)
    pv = pool_v.reshape(n_pool, psz, nk)
    o = pl.pallas_call(
        functools.partial(_diff_sample_body, n_kv=n_kv, group=group, lam_init=lam_init,
                          n_steps=n_steps),
        grid_spec=grid_spec,
        out_shape=jax.ShapeDtypeStruct((B, n_heads, dv), F32),
        compiler_params=_compiler_params(("parallel", "arbitrary", "arbitrary"), 44 << 20),
        name="diff_sample",
    )(page_table, lamv, gsub, qmaps[0], qmaps[1], kn, vn, *([pk] * pages), *([pv] * pages))
    return o.reshape(B, nq).astype(BF16), kn, vn


def _lru_gates(xc, wa_ref, wx_ref, ba_ref, bx_ref, ap_ref):
    xcb = xc.astype(BF16)
    r = jax.nn.sigmoid(jnp.dot(xcb, wa_ref[...].astype(BF16), preferred_element_type=F32)
                       + ba_ref[...])
    ig = jax.nn.sigmoid(jnp.dot(xcb, wx_ref[...].astype(BF16), preferred_element_type=F32)
                        + bx_ref[...])
    log_a = (-LRU_C * r) * jax.nn.softplus(-ap_ref[...])
    a = jnp.exp(log_a)
    b = jnp.sqrt(-jnp.tanh(log_a) * (a * a + 1.0)) * (ig * xc)
    return a, b


def _lru_prompt_body(xb_ref, y_ref, cw_ref, cb_ref, wa_ref, wx_ref, ba_ref, bx_ref, ap_ref,
                     hy_ref, ht_ref, h_s, tail_s):
    t = pl.program_id(2)
    tt, c = xb_ref.shape

    @pl.when(t == 0)
    def _():
        h_s[...] = jnp.zeros_like(h_s)
        tail_s[...] = jnp.zeros_like(tail_s)

    xb = xb_ref[...]
    tail = tail_s[...]
    row8 = lax.broadcasted_iota(jnp.int32, tail.shape, 0)
    xc = cb_ref[...] + xb * cw_ref[CONV_W - 1:CONV_W, :]
    for back in range(1, CONV_W):
        xr = pltpu.roll(xb, back, 0)
        first = jnp.where(row8 < back, pltpu.roll(tail, back, 0), xr[0:8, :])
        shifted = jnp.concatenate([first, xr[8:, :]], axis=0)
        xc = xc + shifted * cw_ref[CONV_W - 1 - back:CONV_W - back, :]
    a, b = _lru_gates(xc, wa_ref, wx_ref, ba_ref, bx_ref, ap_ref)
    row = lax.broadcasted_iota(jnp.int32, (tt, c), 0)
    d = 1
    while d < tt:
        keep = row >= d
        b = jnp.where(keep, a * pltpu.roll(b, d, 0) + b, b)
        a = jnp.where(keep, a * pltpu.roll(a, d, 0), a)
        d *= 2
    hs = a * h_s[...] + b
    hy_ref[...] = (hs * y_ref[...]).astype(hy_ref.dtype)
    h_s[...] = hs[tt - 1:tt, :]
    tail_s[...] = xb[tt - 8:, :]

    @pl.when(t == pl.num_programs(2) - 1)
    def _():
        ht_ref[...] = hs[tt - 1:tt, :]


def _lru_prompt(gy, cw, cb, wa, wx, ba, bx, ap, batch, seq):
    d_rnn = gy.shape[1] // 2
    n_blk, blk = wa.shape[0], wa.shape[1]
    tt = 256
    nt = seq // tt
    chan = lambda r: pl.BlockSpec((r, blk), lambda b, n, t: (0, n))
    wblk = pl.BlockSpec((None, blk, blk), lambda b, n, t: (n, 0, 0))
    hy, ht = pl.pallas_call(
        _lru_prompt_body,
        grid=(batch, n_blk, nt),
        in_specs=[pl.BlockSpec((tt, blk), lambda b, n, t: (b * nt + t, n_blk + n)),
                  pl.BlockSpec((tt, blk), lambda b, n, t: (b * nt + t, n)),
                  chan(CONV_W), chan(1), wblk, wblk, chan(1), chan(1), chan(1)],
        out_specs=[pl.BlockSpec((tt, blk), lambda b, n, t: (b * nt + t, n)),
                   pl.BlockSpec((None, 1, blk), lambda b, n, t: (b, 0, n))],
        out_shape=[jax.ShapeDtypeStruct((batch * seq, d_rnn), BF16),
                   jax.ShapeDtypeStruct((batch, 1, d_rnn), F32)],
        scratch_shapes=[pltpu.VMEM((1, blk), F32), pltpu.VMEM((8, blk), F32)],
        compiler_params=_compiler_params(("parallel", "parallel", "arbitrary"), 16 << 20),
        name="lru_prompt",
    )(gy, gy, cw, cb.reshape(1, d_rnn), wa, wx, ba.reshape(1, d_rnn), bx.reshape(1, d_rnn),
      ap.reshape(1, d_rnn))
    return hy, ht.reshape(batch, d_rnn)


def _lru_sample_body(xb_ref, y_ref, conv_ref, h0_ref, cw_ref, cb_ref, wa_ref, wx_ref, ba_ref,
                     bx_ref, ap_ref, hy_ref, ht_ref):
    xc = cb_ref[...] + conv_ref[0] * cw_ref[0:1, :]
    for tap in range(1, CONV_W - 1):
        xc = xc + conv_ref[tap] * cw_ref[tap:tap + 1, :]
    xc = xc + xb_ref[...] * cw_ref[CONV_W - 1:CONV_W, :]
    a, b = _lru_gates(xc, wa_ref, wx_ref, ba_ref, bx_ref, ap_ref)
    hn = a * h0_ref[...] + b
    ht_ref[...] = hn
    hy_ref[...] = (hn * y_ref[...]).astype(hy_ref.dtype)


def _lru_sample(gy, conv_state, h0, cw, cb, wa, wx, ba, bx, ap):
    B = gy.shape[0]
    d_rnn = gy.shape[1] // 2
    n_blk, blk = wa.shape[0], wa.shape[1]
    chan = lambda r: pl.BlockSpec((r, blk), lambda n: (0, n))
    wblk = pl.BlockSpec((None, blk, blk), lambda n: (n, 0, 0))
    return pl.pallas_call(
        _lru_sample_body,
        grid=(n_blk,),
        in_specs=[pl.BlockSpec((B, blk), lambda n: (0, n_blk + n)),
                  pl.BlockSpec((B, blk), lambda n: (0, n)),
                  pl.BlockSpec((CONV_W - 1, B, blk), lambda n: (0, 0, n)),
                  chan(B), chan(CONV_W), chan(1), wblk, wblk, chan(1), chan(1), chan(1)],
        out_specs=[chan(B), chan(B)],
        out_shape=[jax.ShapeDtypeStruct((B, d_rnn), BF16), jax.ShapeDtypeStruct((B, d_rnn), F32)],
        compiler_params=_compiler_params(("parallel",), 8 << 20),
        name="lru_sample",
    )(gy, gy, conv_state.transpose(1, 0, 2), h0, cw, cb.reshape(1, d_rnn), wa, wx,
      ba.reshape(1, d_rnn), bx.reshape(1, d_rnn), ap.reshape(1, d_rnn))


def _rope_tables(pos):
    half = ROT_DIM // 2
    inv_freq = jnp.exp(-math.log(ROPE_THETA) * jnp.arange(half, dtype=F32) / half)
    ang = pos.astype(F32)[:, None] * inv_freq[None, :]
    cos, sin = jnp.cos(ang), jnp.sin(ang)
    n = pos.shape[0]
    ones = jnp.ones((n, HEAD_DIM - ROT_DIM), F32)
    zeros_h = jnp.zeros((n, half), F32)
    zeros_r = jnp.zeros((n, HEAD_DIM - ROT_DIM), F32)
    cos_t = jnp.concatenate([cos, cos, ones], axis=1)
    sin_lo = jnp.concatenate([-sin, zeros_h, zeros_r], axis=1)
    sin_hi = jnp.concatenate([zeros_h, sin, zeros_r], axis=1)
    return cos_t, sin_lo, sin_hi


def _gate_extras(path, res, j_gate, tm, tn):
    mod = path["mod"]
    r = mod.shape[1]
    return [(res, pl.BlockSpec((tm, tn), lambda m, n, k: (m, n))),
            (mod, pl.BlockSpec((None, r, tn),
                               lambda m, n, k: path["mod_index"](j_gate, m * tm) + (n,)))]


def _project_residual(path, act, w, res, j_gate, name, tk=None):
    tm, tn = path["tm"], (512 if tk is None else 256)
    return _matmul(act, [(w, 0)], _gate_extras(path, res, j_gate, tm, tn), _epi_residual,
                   w.shape[1], F32, tm=tm, tn=tn, tk=tk, name=name)


def _qkv_project(path, h, w, g_q, g_k, n_q_heads, n_k_heads, name):
    tm, tn = path["tm"], 512
    n_out = w.shape[1]
    n_v = n_out - (n_q_heads + n_k_heads) * HEAD_DIM
    gain = jnp.concatenate([jnp.tile(g_q, n_q_heads), jnp.tile(g_k, n_k_heads),
                            jnp.ones((n_v,), F32)]).reshape(1, n_out)
    rope_rows = path["rope"][0].shape[0] // tm
    table = lambda arr: (arr, pl.BlockSpec((tm, HEAD_DIM), lambda m, n, k: (m % rope_rows, 0)))
    extras = [(gain, pl.BlockSpec((1, tn), lambda m, n, k: (0, n)))] + [table(t) for t in path["rope"]]
    epi = functools.partial(_epi_qk_norm_rope,
                            n_qk_tiles=(n_q_heads + n_k_heads) * HEAD_DIM // tn)
    return _matmul(h, [(w, 0)], extras, epi, n_out, F32, tm=tm, tn=tn, name=name)


def _ffn(path, x, g, w_gu, w_down):
    d_ff = w_down.shape[0]
    tm, tn = path["tm"], 256
    h = _modulate(path, x, g, 3, 4)
    act = _matmul(h, [(w_gu, 0), (w_gu, d_ff // tn)], [], _epi_swiglu, d_ff, BF16,
                  tm=tm, tn=tn, name="ffn_gate_up")
    return _project_residual(path, act, w_down, x, 5, "ffn_down", tk=d_ff // 2)


def kernel(x_prompt, x_sample, cache_a_k, cache_a_v, cache_b_k, cache_b_v, state_lru_h, state_lru_conv, page_table, c_prompt, c_sample, w_ada, b_ada, g_norm_mix, g_norm_ffn, a_w_qkv, a_g_q, a_g_k, a_sinks, a_w_o, b_w_qkv, b_g_q, b_g_k, b_lam_q1, b_lam_k1, b_lam_q2, b_lam_k2, b_g_sub, b_w_o, lru_w_in, lru_conv_w, lru_conv_b, lru_wa, lru_ba, lru_wx, lru_bx, lru_a_param, lru_w_out, ffn_w_gu, ffn_w_down):
    batch, seq, d_model = x_prompt.shape
    dec_batch = x_sample.shape[0]
    a_heads, a_kv = a_sinks.shape[1], cache_a_k.shape[3]
    b_kv = cache_b_k.shape[3]
    b_heads = b_w_o.shape[1] // (2 * HEAD_DIM)
    d_rnn = lru_w_out.shape[1]
    w_buf = cache_a_k.shape[2]

    xp = x_prompt.reshape(batch * seq, d_model)
    xs = x_sample.reshape(dec_batch, d_model)
    n_cond = batch + dec_batch
    cond_rows = -(-n_cond // 16) * 16
    cond = jnp.concatenate([c_prompt, c_sample, jnp.zeros((cond_rows - n_cond, d_model), F32)])

    prompt = {"tm": 1024, "tm_rows": 256, "rope": _rope_tables(jnp.arange(seq)),
              "mod_index": lambda j, row0: ((row0 // seq) * 6 + j, 0)}
    sample = {"tm": dec_batch, "tm_rows": dec_batch,
              "rope": _rope_tables(jnp.full((dec_batch,), PAST_LEN)),
              "mod_index": lambda j, row0: (j, 0)}

    outs = {k: [] for k in ("a_kp", "a_vp", "a_ks", "a_vs", "b_kp", "b_vp", "b_ks", "b_vs",
                            "l_hp", "l_cp", "l_hs", "l_cs")}
    for i in range(DEPTH):
        kind, j = i % N_MIXERS, i // N_MIXERS
        mod = _matmul(cond, [(w_ada[i], 0)],
                      [(b_ada[i].reshape(1, -1), pl.BlockSpec((1, 512), lambda m, n, k: (0, n)))],
                      _epi_bias, w_ada.shape[2], F32, tm=cond_rows, tn=512,
                      x_transform=jax.nn.silu, name="ada")
        prompt["mod"] = mod[:batch].reshape(batch * 6, 1, d_model)
        sample["mod"] = mod[batch:n_cond].reshape(dec_batch, 6, d_model).transpose(1, 0, 2)
        hp = _modulate(prompt, xp, g_norm_mix[i], 0, 1)
        hs = _modulate(sample, xs, g_norm_mix[i], 0, 1)
        if kind == 0:
            nq, nk = a_heads * HEAD_DIM, a_kv * HEAD_DIM
            qkv_p = _qkv_project(prompt, hp, a_w_qkv[j], a_g_q[j], a_g_k[j], a_heads, a_kv, "a_qkv")
            qkv_s = _qkv_project(sample, hs, a_w_qkv[j], a_g_q[j], a_g_k[j], a_heads, a_kv, "a_qkv")
            op = _swa_prompt(qkv_p, a_sinks[j], batch, seq, a_heads, a_kv)
            os_, kn, vn = _swa_sample(qkv_s, cache_a_k[j], cache_a_v[j], a_sinks[j], a_heads, a_kv)
            w_o = a_w_o[j]
            kv_p = qkv_p.reshape(batch, seq, -1)[:, seq - w_buf:, nq:]
            outs["a_kp"].append(kv_p[..., :nk].reshape(batch, w_buf, a_kv, HEAD_DIM))
            outs["a_vp"].append(kv_p[..., nk:].reshape(batch, w_buf, a_kv, HEAD_DIM))
            outs["a_ks"].append(jnp.concatenate([cache_a_k[j][:, 1:], kn[:, None]], axis=1))
            outs["a_vs"].append(jnp.concatenate([cache_a_v[j][:, 1:], vn[:, None]], axis=1))
        elif kind == 1:
            lam_init = 0.8 - 0.6 * math.exp(-0.3 * i)
            dv = 2 * HEAD_DIM
            nq, nk = b_heads * dv, b_kv * dv
            lamv = jnp.stack([b_lam_q1[j], b_lam_k1[j], b_lam_q2[j], b_lam_k2[j]])
            gsub = b_g_sub[j].reshape(1, dv)
            qkv_p = _qkv_project(prompt, hp, b_w_qkv[j], b_g_q[j], b_g_k[j], 2 * b_heads, 2 * b_kv, "b_qkv")
            qkv_s = _qkv_project(sample, hs, b_w_qkv[j], b_g_q[j], b_g_k[j], 2 * b_heads, 2 * b_kv, "b_qkv")
            op = _diff_prompt(qkv_p, lamv, gsub, lam_init, batch, seq, b_heads, b_kv)
            os_, kn, vn = _diff_sample(qkv_s, cache_b_k[j], cache_b_v[j], page_table, lamv, gsub,
                                       lam_init, b_heads, b_kv)
            w_o = b_w_o[j]
            outs["b_kp"].append(qkv_p[:, nq:nq + nk].reshape(batch, seq, b_kv, dv))
            outs["b_vp"].append(qkv_p[:, nq + nk:].reshape(batch, seq, b_kv, dv))
            outs["b_ks"].append(kn.reshape(dec_batch, 1, b_kv, dv))
            outs["b_vs"].append(vn.reshape(dec_batch, 1, b_kv, dv))
        else:
            gelu_tiles = functools.partial(_epi_gelu_first_tiles, n_gelu_tiles=d_rnn // 512)
            lru_w = (lru_conv_w[j], lru_conv_b[j], lru_wa[j], lru_wx[j], lru_ba[j], lru_bx[j],
                     lru_a_param[j])
            gy_p = _matmul(hp, [(lru_w_in[j], 0)], [], gelu_tiles, 2 * d_rnn, F32,
                           tm=prompt["tm"], tn=512, name="lru_in")
            gy_s = _matmul(hs, [(lru_w_in[j], 0)], [], gelu_tiles, 2 * d_rnn, F32,
                           tm=sample["tm"], tn=512, name="lru_in")
            op, ht_p = _lru_prompt(gy_p, *lru_w, batch, seq)
            os_, ht_s = _lru_sample(gy_s, state_lru_conv[j], state_lru_h[j], *lru_w)
            w_o = lru_w_out[j]
            outs["l_hp"].append(ht_p)
            outs["l_cp"].append(gy_p.reshape(batch, seq, -1)[:, seq - (CONV_W - 1):, d_rnn:])
            outs["l_hs"].append(ht_s)
            outs["l_cs"].append(jnp.concatenate([state_lru_conv[j][:, 1:], gy_s[:, None, d_rnn:]],
                                                axis=1))
        xp = _project_residual(prompt, op, w_o, xp, 2, "mixer_out")
        xs = _project_residual(sample, os_, w_o, xs, 2, "mixer_out")
        xp = _ffn(prompt, xp, g_norm_ffn[i], ffn_w_gu[i], ffn_w_down[i])
        xs = _ffn(sample, xs, g_norm_ffn[i], ffn_w_gu[i], ffn_w_down[i])

    stack = lambda key: jnp.stack(outs[key])
    return (xp.reshape(batch, seq, d_model), xs.reshape(dec_batch, 1, d_model),
            stack("a_kp"), stack("a_vp"), stack("a_ks"), stack("a_vs"),
            stack("b_kp"), stack("b_vp"), stack("b_ks"), stack("b_vs"),
            stack("l_hp"), stack("l_cp"), stack("l_hs"), stack("l_cs"))
```
